```python
import jax, jax.numpy as jnp
from jax import lax
import numpy as np

D_MODEL = 1024
BATCH = 8
SEQ = 2048
DEPTH = 1

HEAD_DIM = 64
N_Q_HEADS = 8
N_KV_HEADS = 2
ATTN_WIDTH = N_Q_HEADS * HEAD_DIM
KV_WIDTH = N_KV_HEADS * HEAD_DIM
CONV_WIDTH = D_MODEL - ATTN_WIDTH
MIX_WIDTH = ATTN_WIDTH + CONV_WIDTH
CONV_KERNEL = 31
WINDOW = 128
BLOCK = 128
ROPE_THETA = 10000.0
EPS = 1e-6
IN_WIDTH = ATTN_WIDTH + 2 * KV_WIDTH + ATTN_WIDTH + 2 * CONV_WIDTH + CONV_WIDTH

kernel_name = "hybrid_swa_sink_conformer_adaln"


def _rms_norm(t, w):
    tf = t.astype(jnp.float32)
    y = tf * lax.rsqrt(jnp.mean(tf * tf, axis=-1, keepdims=True) + EPS)
    return (y * w.astype(jnp.float32)).astype(t.dtype)


def _layer_norm(t, w, b):
    tf = t.astype(jnp.float32)
    mu = jnp.mean(tf, axis=-1, keepdims=True)
    var = jnp.mean(jnp.square(tf - mu), axis=-1, keepdims=True)
    y = (tf - mu) * lax.rsqrt(var + EPS)
    return (y * w.astype(jnp.float32) + b.astype(jnp.float32)).astype(t.dtype)


def _rope_tables(seq_len):
    inv = ROPE_THETA ** (-jnp.arange(0, HEAD_DIM, 2, dtype=jnp.float32) / HEAD_DIM)
    ang = jnp.arange(seq_len, dtype=jnp.float32)[:, None] * inv[None, :]
    return jnp.cos(ang), jnp.sin(ang)


def _apply_rope(t, cos, sin):
    tf = t.astype(jnp.float32)
    t1, t2 = jnp.split(tf, 2, axis=-1)
    c_, s_ = cos[None, :, None, :], sin[None, :, None, :]
    return jnp.concatenate([t1 * c_ - t2 * s_, t2 * c_ + t1 * s_], axis=-1).astype(t.dtype)


def _sliding_window_sink_attention(q, k, v, sinks):
    B, S, _, dh = q.shape
    nb = S // BLOCK
    G = N_Q_HEADS // N_KV_HEADS
    qb = q.reshape(B, nb, BLOCK, N_KV_HEADS, G, dh)

    def band(t):
        tb = t.reshape(B, nb, BLOCK, N_KV_HEADS, dh)
        prev = jnp.concatenate([jnp.zeros_like(tb[:, :1]), tb[:, :-1]], axis=1)
        return jnp.concatenate([prev, tb], axis=2)

    kb, vb = band(k), band(v)
    s = jnp.einsum('bnqhgd,bnkhd->bnhgqk', qb, kb,
                   preferred_element_type=jnp.float32) * (dh ** -0.5)
    qi = jnp.arange(BLOCK)[:, None]
    kj = jnp.arange(2 * BLOCK)[None, :]
    dist = qi + BLOCK - kj
    local = (dist >= 0) & (dist < WINDOW)
    exists = (jnp.arange(nb)[:, None, None] > 0) | (kj[None] >= BLOCK)
    mask = local[None] & exists
    s = jnp.where(mask[None, :, None, None], s, -jnp.inf)
    sink = sinks.astype(jnp.float32).reshape(1, 1, N_KV_HEADS, G, 1, 1)
    m = jnp.maximum(jnp.max(s, axis=-1, keepdims=True), sink)
    p = jnp.exp(s - m)
    p = p / (jnp.sum(p, axis=-1, keepdims=True) + jnp.exp(sink - m))
    o = jnp.einsum('bnhgqk,bnkhd->bnqhgd', p.astype(v.dtype), vb)
    return o.reshape(B, S, N_Q_HEADS * dh)


def _conformer_conv(u, conv_w, conv_b, ln_w, ln_b):
    a, g = jnp.split(u, 2, axis=-1)
    z = a * jax.nn.sigmoid(g)
    z = lax.conv_general_dilated(
        z, conv_w[:, None, :], window_strides=(1,),
        padding=[(CONV_KERNEL - 1, 0)],
        dimension_numbers=('NWC', 'WIO', 'NWC'),
        feature_group_count=CONV_WIDTH) + conv_b
    z = _layer_norm(z, ln_w, ln_b)
    return jax.nn.silu(z)


def setup_inputs(seed: int = 0) -> dict:
    key = jax.random.key(seed)
    ks = jax.random.split(key, 16)
    f32 = jnp.float32
    n = lambda k, shape, s: jax.random.normal(k, shape, f32) * s
    return {
        "x": n(ks[0], (BATCH, SEQ, D_MODEL), 1.0),
        "c": n(ks[1], (BATCH, D_MODEL), 1.0),
        "w_ada": n(ks[2], (DEPTH, D_MODEL, 3 * D_MODEL), 0.5 * D_MODEL ** -0.5),
        "b_ada": n(ks[3], (DEPTH, 3 * D_MODEL), 0.02),
        "norm_w": 1.0 + n(ks[4], (DEPTH, D_MODEL), 0.02),
        "w_in": n(ks[5], (DEPTH, D_MODEL, IN_WIDTH), D_MODEL ** -0.5),
        "q_norm_w": 1.0 + n(ks[6], (DEPTH, HEAD_DIM), 0.02),
        "k_norm_w": 1.0 + n(ks[7], (DEPTH, HEAD_DIM), 0.02),
        "sinks": n(ks[8], (DEPTH, N_Q_HEADS), 0.5),
        "conv_w": n(ks[9], (DEPTH, CONV_KERNEL, CONV_WIDTH), CONV_KERNEL ** -0.5),
        "conv_b": n(ks[10], (DEPTH, CONV_WIDTH), 0.02),
        "ln_w": 1.0 + n(ks[11], (DEPTH, CONV_WIDTH), 0.02),
        "ln_b": n(ks[12], (DEPTH, CONV_WIDTH), 0.02),
        "w_out": n(ks[13], (DEPTH, MIX_WIDTH, D_MODEL), MIX_WIDTH ** -0.5),
    }


def reference(x, c, w_ada, b_ada, norm_w, w_in, q_norm_w, k_norm_w, sinks,
              conv_w, conv_b, ln_w, ln_b, w_out):
    B, S, _ = x.shape
    cos, sin = _rope_tables(S)
    c_act = jax.nn.silu(c)
    splits = (ATTN_WIDTH,
              ATTN_WIDTH + KV_WIDTH,
              ATTN_WIDTH + 2 * KV_WIDTH,
              2 * ATTN_WIDTH + 2 * KV_WIDTH,
              2 * ATTN_WIDTH + 2 * KV_WIDTH + 2 * CONV_WIDTH)
    for l in range(DEPTH):
        shift, scale, gate = jnp.split(c_act @ w_ada[l] + b_ada[l], 3, axis=-1)
        h = _rms_norm(x, norm_w[l]) * (1.0 + scale[:, None, :]) + shift[:, None, :]
        proj = h @ w_in[l]
        q, k, v, gate_a, u, gate_b = jnp.split(proj, splits, axis=-1)
        q = q.reshape(B, S, N_Q_HEADS, HEAD_DIM)
        k = k.reshape(B, S, N_KV_HEADS, HEAD_DIM)
        v = v.reshape(B, S, N_KV_HEADS, HEAD_DIM)
        q = _apply_rope(_rms_norm(q, q_norm_w[l]), cos, sin)
        k = _apply_rope(_rms_norm(k, k_norm_w[l]), cos, sin)
        y_a = _sliding_window_sink_attention(q, k, v, sinks[l]) * jax.nn.silu(gate_a)
        y_b = _conformer_conv(u, conv_w[l], conv_b[l], ln_w[l], ln_b[l]) * jax.nn.silu(gate_b)
        y = jnp.concatenate([y_a, y_b], axis=-1) @ w_out[l]
        x = x + gate[:, None, :] * y
    return x
```

```python
import functools

import jax
import jax.numpy as jnp
from jax import lax
from jax.experimental import pallas as pl
from jax.experimental.pallas import tpu as pltpu

D_MODEL = 1024
HEAD_DIM = 64
HEAD_SHIFT = HEAD_DIM.bit_length() - 1
N_Q_HEADS = 8
N_KV_HEADS = 2
GROUP = N_Q_HEADS // N_KV_HEADS
ATTN_WIDTH = N_Q_HEADS * HEAD_DIM
KV_WIDTH = N_KV_HEADS * HEAD_DIM
CONV_WIDTH = D_MODEL - ATTN_WIDTH
CONV_KERNEL = 31
BLOCK = 128
ROPE_THETA = 10000.0
EPS = 1e-6
IN_WIDTH = 2 * ATTN_WIDTH + 2 * KV_WIDTH + 3 * CONV_WIDTH

Q_OFF = 0
K_OFF = ATTN_WIDTH
V_OFF = K_OFF + KV_WIDTH
GA_OFF = V_OFF + KV_WIDTH
UA_OFF = GA_OFF + ATTN_WIDTH
UG_OFF = UA_OFF + CONV_WIDTH
GB_OFF = UG_OFF + CONV_WIDTH

LANES = 128
SUBLANES = 8
MXU_N = 256
TS = 512
ZPAD = 32
ZSHIFT = ZPAD - (CONV_KERNEL - 1)
CONV_GROUPS = CONV_WIDTH // LANES
NORM_ROWS = 64
CONV_ROWS = 32
ADA_TILE = 512
VMEM_LIMIT_BYTES = 56 * 1024 * 1024

F32 = jnp.float32
BF16 = jnp.bfloat16


def _silu(t):
    return t / (1.0 + jnp.exp(-t))


def _adaln_kernel(c_ref, w_ref, b_ref, o_ref):
    c = c_ref[...]
    o_ref[...] = jnp.dot(_silu(c).astype(BF16), w_ref[...].astype(BF16),
                         preferred_element_type=F32) + b_ref[...]


def _adaln(c, w_ada, b_ada):
    b, d = c.shape
    n = w_ada.shape[1]
    return pl.pallas_call(
        _adaln_kernel,
        grid=(n // ADA_TILE,),
        in_specs=[pl.BlockSpec((b, d), lambda j: (0, 0)),
                  pl.BlockSpec((d, ADA_TILE), lambda j: (0, j)),
                  pl.BlockSpec((1, ADA_TILE), lambda j: (0, j))],
        out_specs=pl.BlockSpec((b, ADA_TILE), lambda j: (0, j)),
        out_shape=jax.ShapeDtypeStruct((b, n), F32),
        name="adaln",
    )(c, w_ada, b_ada.reshape(1, n))


def _rope(t, cos, sin_signed, lo_half):
    partner = jnp.where(lo_half, pltpu.roll(t, LANES - HEAD_DIM // 2, 1), pltpu.roll(t, HEAD_DIM // 2, 1))
    return t * cos + partner * sin_signed


def _layer_kernel(sinks_ref, x_ref, mod_ref, nw_ref, win_ref, qw_ref, kw_ref, cos_ref, sin_ref,
                  cw_ref, cb_ref, lnw_ref, lnb_ref, wout_ref, o_ref,
                  h_s, proj_s, qa_s, qb_s, k2_s, va_s, vb_s, y_s, z_s, wb_s, e_s):
    b = pl.program_id(0)
    i = pl.program_id(1)
    nblk = TS // BLOCK

    @pl.when((b == 0) & (i == 0))
    def _init_constants():
        r = lax.broadcasted_iota(jnp.int32, (ATTN_WIDTH, ATTN_WIDTH), 0) >> HEAD_SHIFT
        c = lax.broadcasted_iota(jnp.int32, (ATTN_WIDTH, ATTN_WIDTH), 1) >> HEAD_SHIFT
        e_s[...] = jnp.where(r == c, 1.0, 0.0).astype(BF16)
        for k in range(CONV_KERNEL):
            wb_s[k] = jnp.broadcast_to(cw_ref[k:k + 1, :], (SUBLANES, CONV_WIDTH))

    @pl.when(i == 0)
    def _reset_carry():
        zkv = jnp.zeros((N_KV_HEADS, BLOCK, LANES), BF16)
        k2_s[:, 0:BLOCK, :] = zkv
        va_s[:, 0:BLOCK, :] = zkv
        vb_s[:, 0:BLOCK, :] = zkv
        for cg in range(CONV_GROUPS):
            z_s[cg, pl.ds(0, ZPAD, stride=2), :] = jnp.zeros((ZPAD, LANES), F32)

    shift = mod_ref[0:1, :]
    scale = mod_ref[1:2, :]
    gate = mod_ref[2:3, :]

    gain = nw_ref[...] * (1.0 + scale)

    def norm_body(c, carry):
        rows = pl.ds(pl.multiple_of(c * NORM_ROWS, NORM_ROWS), NORM_ROWS)
        x = x_ref[rows, :]
        r = lax.rsqrt(jnp.mean(x * x, axis=-1, keepdims=True) + EPS)
        h_s[rows, :] = ((x * r) * gain + shift).astype(BF16)
        return carry

    lax.fori_loop(0, TS // NORM_ROWS, norm_body, 0)

    for n in range(IN_WIDTH // MXU_N):
        cols = slice(n * MXU_N, (n + 1) * MXU_N)
        proj_s[:, cols] = jnp.dot(h_s[...], win_ref[:, cols], preferred_element_type=F32)

    lane = lax.broadcasted_iota(jnp.int32, (BLOCK, LANES), 1)
    head_a = lane < HEAD_DIM
    lo_half = (lane & (HEAD_DIM - 1)) < (HEAD_DIM // 2)
    inv_dh = 1.0 / HEAD_DIM
    q_scale = HEAD_DIM ** -0.5

    def qkv_body(c, carry):
        r0 = pl.multiple_of(c * BLOCK, BLOCK)
        rows = pl.ds(r0, BLOCK)
        krows = pl.ds(r0 + BLOCK, BLOCK)
        cos = cos_ref[rows, :]
        sin = sin_ref[rows, :]
        q = proj_s[rows, Q_OFF:Q_OFF + ATTN_WIDTH]
        ssq = jnp.dot((q * q).astype(BF16), e_s[...], preferred_element_type=F32)
        qn = q * lax.rsqrt(ssq * inv_dh + EPS) * qw_ref[...]
        for p in range(ATTN_WIDTH // LANES):
            cols = slice(p * LANES, (p + 1) * LANES)
            qr = _rope(qn[:, cols], cos, sin, lo_half) * q_scale
            qa_s[rows, cols] = jnp.where(head_a, qr, 0.0).astype(BF16)
            qb_s[rows, cols] = jnp.where(head_a, 0.0, qr).astype(BF16)
        k = proj_s[rows, K_OFF:K_OFF + KV_WIDTH]
        ssk = jnp.dot((k * k).astype(BF16), e_s[0:KV_WIDTH, 0:KV_WIDTH], preferred_element_type=F32)
        kn = k * lax.rsqrt(ssk * inv_dh + EPS) * kw_ref[...]
        kf = _rope(kn, cos, sin, lo_half)
        kr = pltpu.roll(kf, HEAD_DIM, 1)
        k2_s[0, krows, :] = jnp.where(head_a, kf, kr).astype(BF16)
        k2_s[1, krows, :] = jnp.where(head_a, kr, kf).astype(BF16)
        v = proj_s[rows, V_OFF:V_OFF + KV_WIDTH]
        vr = pltpu.roll(v, HEAD_DIM, 1)
        va_s[0, krows, :] = jnp.where(head_a, v, 0.0).astype(BF16)
        vb_s[0, krows, :] = jnp.where(head_a, 0.0, vr).astype(BF16)
        va_s[1, krows, :] = jnp.where(head_a, vr, 0.0).astype(BF16)
        vb_s[1, krows, :] = jnp.where(head_a, 0.0, v).astype(BF16)
        ua = proj_s[rows, UA_OFF:UA_OFF + CONV_WIDTH]
        ug = proj_s[rows, UG_OFF:UG_OFF + CONV_WIDTH]
        z = ua / (1.0 + jnp.exp(-ug))
        for cg in range(CONV_GROUPS):
            z_s[cg, pl.ds(2 * (r0 + ZPAD), BLOCK, stride=2), :] = z[:, cg * LANES:(cg + 1) * LANES]
        return carry

    lax.fori_loop(0, nblk, qkv_body, 0)

    qi = lax.broadcasted_iota(jnp.int32, (BLOCK, 2 * BLOCK), 0)
    kj = lax.broadcasted_iota(jnp.int32, (BLOCK, 2 * BLOCK), 1)
    local = ((kj < BLOCK) & (kj > qi)) | ((kj >= BLOCK) & (kj - BLOCK <= qi))
    row_head = (0, 2, 1, 3)

    def attn_body(j, carry):
        r0 = pl.multiple_of(j * BLOCK, BLOCK)
        rows = pl.ds(r0, BLOCK)
        keys = pl.ds(r0, 2 * BLOCK)
        first_key = jnp.where(i * nblk + j == 0, BLOCK, 0)
        mask = local & (kj >= first_key)
        for g in range(N_KV_HEADS):
            c0 = g * 2 * LANES
            qs = jnp.concatenate([qa_s[rows, c0:c0 + LANES], qa_s[rows, c0 + LANES:c0 + 2 * LANES],
                                  qb_s[rows, c0:c0 + LANES], qb_s[rows, c0 + LANES:c0 + 2 * LANES]], axis=0)
            s = lax.dot_general(qs, k2_s[g, keys, :], (((1,), (1,)), ((), ())),
                                preferred_element_type=F32)
            ps, invs = [], []
            for hb in range(GROUP):
                sink = sinks_ref[g * GROUP + row_head[hb]]
                sb = jnp.where(mask, s[hb * BLOCK:(hb + 1) * BLOCK], -jnp.inf)
                m = jnp.maximum(jnp.max(sb, axis=-1, keepdims=True), sink)
                p = jnp.exp(sb - m)
                invs.append(1.0 / (jnp.sum(p, axis=-1, keepdims=True) + jnp.exp(sink - m)))
                ps.append(p.astype(BF16))
            o = (jnp.dot(jnp.concatenate(ps[0:2], axis=0), va_s[g, keys, :], preferred_element_type=F32)
                 + jnp.dot(jnp.concatenate(ps[2:4], axis=0), vb_s[g, keys, :], preferred_element_type=F32))
            for pr in range(2):
                cols = slice(c0 + pr * LANES, c0 + (pr + 1) * LANES)
                inv = jnp.where(head_a, invs[pr], invs[2 + pr])
                ga = proj_s[rows, GA_OFF + c0 + pr * LANES:GA_OFF + c0 + (pr + 1) * LANES]
                y_s[rows, cols] = (o[pr * BLOCK:(pr + 1) * BLOCK] * inv * _silu(ga)).astype(BF16)
        return carry

    lax.fori_loop(0, nblk, attn_body, 0)

    cbias = jnp.broadcast_to(cb_ref[...], (SUBLANES, CONV_WIDTH))
    pieces = CONV_ROWS // SUBLANES

    def conv_body(c, carry):
        r0 = pl.multiple_of(c * CONV_ROWS, CONV_ROWS)
        groups = []
        for cg in range(CONV_GROUPS):
            cols = slice(cg * LANES, (cg + 1) * LANES)
            acc = [cbias[:, cols]] * pieces
            for k in range(CONV_KERNEL):
                wk = wb_s[k, :, cols]
                for r in range(pieces):
                    t0 = r0 + r * SUBLANES + ZSHIFT + k
                    acc[r] = acc[r] + z_s[cg, pl.ds(2 * t0, SUBLANES, stride=2), :] * wk
            groups.append(jnp.concatenate(acc, axis=0))
        zc = jnp.concatenate(groups, axis=1)
        mu = jnp.mean(zc, axis=-1, keepdims=True)
        d = zc - mu
        var = jnp.mean(d * d, axis=-1, keepdims=True)
        ln = d * lax.rsqrt(var + EPS) * lnw_ref[...] + lnb_ref[...]
        rows = pl.ds(r0, CONV_ROWS)
        gb = proj_s[rows, GB_OFF:GB_OFF + CONV_WIDTH]
        y_s[rows, ATTN_WIDTH:ATTN_WIDTH + CONV_WIDTH] = (_silu(ln) * _silu(gb)).astype(BF16)
        return carry

    lax.fori_loop(0, TS // CONV_ROWS, conv_body, 0)

    for n in range(D_MODEL // MXU_N):
        cols = slice(n * MXU_N, (n + 1) * MXU_N)
        out = jnp.dot(y_s[...], wout_ref[:, cols], preferred_element_type=F32)
        o_ref[:, cols] = x_ref[:, cols] + gate[:, cols] * out

    k2_s[:, 0:BLOCK, :] = k2_s[:, TS:TS + BLOCK, :]
    va_s[:, 0:BLOCK, :] = va_s[:, TS:TS + BLOCK, :]
    vb_s[:, 0:BLOCK, :] = vb_s[:, TS:TS + BLOCK, :]
    for cg in range(CONV_GROUPS):
        z_s[cg, pl.ds(0, ZPAD, stride=2), :] = z_s[cg, pl.ds(2 * TS, ZPAD, stride=2), :]


def _rope_tables(seq_len):
    inv = ROPE_THETA ** (-jnp.arange(0, HEAD_DIM, 2, dtype=F32) / HEAD_DIM)
    ang = jnp.arange(seq_len, dtype=F32)[:, None] * inv[None, :]
    cos, sin = jnp.cos(ang), jnp.sin(ang)
    cos_l = jnp.tile(cos, (1, 2 * LANES // HEAD_DIM))
    sin_l = jnp.tile(jnp.concatenate([-sin, sin], axis=-1), (1, LANES // HEAD_DIM))
    return cos_l, sin_l


def kernel(x, c, w_ada, b_ada, norm_w, w_in, q_norm_w, k_norm_w, sinks, conv_w, conv_b, ln_w, ln_b, w_out):
    bsz, seq, d = x.shape
    assert d == D_MODEL and seq % TS == 0 and w_ada.shape[0] == 1
    cos_l, sin_l = _rope_tables(seq)
    mod = _adaln(c, w_ada[0], b_ada[0]).reshape(bsz, 3, d)
    win = w_in[0].astype(BF16)
    wout = w_out[0].astype(BF16)
    qw = jnp.tile(q_norm_w[0], N_Q_HEADS).reshape(1, ATTN_WIDTH)
    kw = jnp.tile(k_norm_w[0], N_KV_HEADS).reshape(1, KV_WIDTH)

    const = lambda shape: pl.BlockSpec(shape, lambda b, i: (0,) * len(shape))
    return pl.pallas_call(
        _layer_kernel,
        grid=(bsz, seq // TS),
        in_specs=[
            pl.BlockSpec(memory_space=pltpu.SMEM),
            pl.BlockSpec((None, TS, d), lambda b, i: (b, i, 0)),
            pl.BlockSpec((None, 3, d), lambda b, i: (b, 0, 0)),
            const((1, d)),
            const((d, IN_WIDTH)),
            const((1, ATTN_WIDTH)),
            const((1, KV_WIDTH)),
            pl.BlockSpec((TS, LANES), lambda b, i: (i, 0)),
            pl.BlockSpec((TS, LANES), lambda b, i: (i, 0)),
            const((CONV_KERNEL, CONV_WIDTH)),
            const((1, CONV_WIDTH)),
            const((1, CONV_WIDTH)),
            const((1, CONV_WIDTH)),
            const((d, d)),
        ],
        out_specs=pl.BlockSpec((None, TS, d), lambda b, i: (b, i, 0)),
        out_shape=jax.ShapeDtypeStruct((bsz, seq, d), F32),
        scratch_shapes=[
            pltpu.VMEM((TS, d), BF16),
            pltpu.VMEM((TS, IN_WIDTH), F32),
            pltpu.VMEM((TS, ATTN_WIDTH), BF16),
            pltpu.VMEM((TS, ATTN_WIDTH), BF16),
            pltpu.VMEM((N_KV_HEADS, TS + BLOCK, LANES), BF16),
            pltpu.VMEM((N_KV_HEADS, TS + BLOCK, LANES), BF16),
            pltpu.VMEM((N_KV_HEADS, TS + BLOCK, LANES), BF16),
            pltpu.VMEM((TS, d), BF16),
            pltpu.VMEM((CONV_GROUPS, 2 * (TS + ZPAD), LANES), F32),
            pltpu.VMEM((CONV_KERNEL, SUBLANES, CONV_WIDTH), F32),
            pltpu.VMEM((ATTN_WIDTH, ATTN_WIDTH), BF16),
        ],
        compiler_params=pltpu.CompilerParams(
            dimension_semantics=("arbitrary", "arbitrary"),
            vmem_limit_bytes=VMEM_LIMIT_BYTES),
        name="hybrid_layer",
    )(sinks[0], x, mod, norm_w, win, qw, kw, cos_l, sin_l, conv_w[0], conv_b, ln_w, ln_b, wout)
```

```python
import functools

import jax
import jax.numpy as jnp
from jax import lax
from jax.experimental import pallas as pl
from jax.experimental.pallas import tpu as pltpu

D_MODEL = 1024
HEAD_DIM = 64
HEAD_SHIFT = HEAD_DIM.bit_length() - 1
N_Q_HEADS = 8
N_KV_HEADS = 2
GROUP = N_Q_HEADS // N_KV_HEADS
ATTN_WIDTH = N_Q_HEADS * HEAD_DIM
KV_WIDTH = N_KV_HEADS * HEAD_DIM
CONV_WIDTH = D_MODEL - ATTN_WIDTH
CONV_KERNEL = 31
BLOCK = 128
ROPE_THETA = 10000.0
EPS = 1e-6
IN_WIDTH = 2 * ATTN_WIDTH + 2 * KV_WIDTH + 3 * CONV_WIDTH

Q_OFF = 0
K_OFF = ATTN_WIDTH
V_OFF = K_OFF + KV_WIDTH
GA_OFF = V_OFF + KV_WIDTH
UA_OFF = GA_OFF + ATTN_WIDTH
UG_OFF = UA_OFF + CONV_WIDTH
GB_OFF = UG_OFF + CONV_WIDTH

LANES = 128
SUBLANES = 8
MXU_N = 256
TS = 512
ZPAD = 32
ZSHIFT = ZPAD - (CONV_KERNEL - 1)
CONV_GROUPS = CONV_WIDTH // LANES
NORM_ROWS = 64
CONV_ROWS = 32
ADA_TILE = 512
VMEM_LIMIT_BYTES = 56 * 1024 * 1024

F32 = jnp.float32
BF16 = jnp.bfloat16


def _silu(t):
    return t / (1.0 + jnp.exp(-t))


def _adaln_kernel(c_ref, w_ref, b_ref, o_ref):
    c = c_ref[...]
    o_ref[...] = jnp.dot(_silu(c).astype(BF16), w_ref[...].astype(BF16),
                         preferred_element_type=F32) + b_ref[...]


def _adaln(c, w_ada, b_ada):
    b, d = c.shape
    n = w_ada.shape[1]
    return pl.pallas_call(
        _adaln_kernel,
        grid=(n // ADA_TILE,),
        in_specs=[pl.BlockSpec((b, d), lambda j: (0, 0)),
                  pl.BlockSpec((d, ADA_TILE), lambda j: (0, j)),
                  pl.BlockSpec((1, ADA_TILE), lambda j: (0, j))],
        out_specs=pl.BlockSpec((b, ADA_TILE), lambda j: (0, j)),
        out_shape=jax.ShapeDtypeStruct((b, n), F32),
        name="adaln",
    )(c, w_ada, b_ada.reshape(1, n))


def _rope(t, cos, sin_signed, lo_half):
    partner = jnp.where(lo_half, pltpu.roll(t, LANES - HEAD_DIM // 2, 1), pltpu.roll(t, HEAD_DIM // 2, 1))
    return t * cos + partner * sin_signed


def _layer_kernel(sinks_ref, x_ref, mod_ref, nw_ref, win_ref, qw_ref, kw_ref, cos_ref, sin_ref,
                  cw_ref, cb_ref, lnw_ref, lnb_ref, wout_ref, o_ref,
                  h_s, proj_s, qa_s, qb_s, k2_s, va_s, vb_s, y_s, z_s, wb_s, e_s):
    b = pl.program_id(0)
    i = pl.program_id(1)
    nblk = TS // BLOCK

    @pl.when((b == 0) & (i == 0))
    def _init_constants():
        r = lax.broadcasted_iota(jnp.int32, (ATTN_WIDTH, ATTN_WIDTH), 0) >> HEAD_SHIFT
        c = lax.broadcasted_iota(jnp.int32, (ATTN_WIDTH, ATTN_WIDTH), 1) >> HEAD_SHIFT
        e_s[...] = jnp.where(r == c, 1.0, 0.0).astype(BF16)
        for k in range(CONV_KERNEL):
            wb_s[k] = jnp.broadcast_to(cw_ref[k:k + 1, :], (SUBLANES, CONV_WIDTH))

    @pl.when(i == 0)
    def _reset_carry():
        zkv = jnp.zeros((N_KV_HEADS, BLOCK, LANES), BF16)
        k2_s[:, 0:BLOCK, :] = zkv
        va_s[:, 0:BLOCK, :] = zkv
        vb_s[:, 0:BLOCK, :] = zkv
        for cg in range(CONV_GROUPS):
            z_s[cg, pl.ds(0, ZPAD, stride=2), :] = jnp.zeros((ZPAD, LANES), F32)

    shift = mod_ref[0:1, :]
    scale = mod_ref[1:2, :]
    gate = mod_ref[2:3, :]

    gain = nw_ref[...] * (1.0 + scale)

    for c in range(TS // NORM_ROWS):
        rows = pl.ds(c * NORM_ROWS, NORM_ROWS)
        x = x_ref[rows, :]
        r = lax.rsqrt(jnp.mean(x * x, axis=-1, keepdims=True) + EPS)
        h_s[rows, :] = ((x * r) * gain + shift).astype(BF16)

    for n in range(IN_WIDTH // MXU_N):
        cols = slice(n * MXU_N, (n + 1) * MXU_N)
        proj_s[:, cols] = jnp.dot(h_s[...], win_ref[:, cols], preferred_element_type=F32)

    lane = lax.broadcasted_iota(jnp.int32, (BLOCK, LANES), 1)
    head_a = lane < HEAD_DIM
    lo_half = (lane & (HEAD_DIM - 1)) < (HEAD_DIM // 2)
    inv_dh = 1.0 / HEAD_DIM
    q_scale = HEAD_DIM ** -0.5

    for c in range(nblk):
        r0 = c * BLOCK
        rows = pl.ds(r0, BLOCK)
        krows = pl.ds(r0 + BLOCK, BLOCK)
        cos = cos_ref[rows, :]
        sin = sin_ref[rows, :]
        q = proj_s[rows, Q_OFF:Q_OFF + ATTN_WIDTH]
        ssq = jnp.dot((q * q).astype(BF16), e_s[...], preferred_element_type=F32)
        qn = q * lax.rsqrt(ssq * inv_dh + EPS) * qw_ref[...]
        for p in range(ATTN_WIDTH // LANES):
            cols = slice(p * LANES, (p + 1) * LANES)
            qr = _rope(qn[:, cols], cos, sin, lo_half) * q_scale
            qa_s[rows, cols] = jnp.where(head_a, qr, 0.0).astype(BF16)
            qb_s[rows, cols] = jnp.where(head_a, 0.0, qr).astype(BF16)
        k = proj_s[rows, K_OFF:K_OFF + KV_WIDTH]
        ssk = jnp.dot((k * k).astype(BF16), e_s[0:KV_WIDTH, 0:KV_WIDTH], preferred_element_type=F32)
        kn = k * lax.rsqrt(ssk * inv_dh + EPS) * kw_ref[...]
        kf = _rope(kn, cos, sin, lo_half)
        kr = pltpu.roll(kf, HEAD_DIM, 1)
        k2_s[0, krows, :] = jnp.where(head_a, kf, kr).astype(BF16)
        k2_s[1, krows, :] = jnp.where(head_a, kr, kf).astype(BF16)
        v = proj_s[rows, V_OFF:V_OFF + KV_WIDTH]
        vr = pltpu.roll(v, HEAD_DIM, 1)
        va_s[0, krows, :] = jnp.where(head_a, v, 0.0).astype(BF16)
        vb_s[0, krows, :] = jnp.where(head_a, 0.0, vr).astype(BF16)
        va_s[1, krows, :] = jnp.where(head_a, vr, 0.0).astype(BF16)
        vb_s[1, krows, :] = jnp.where(head_a, 0.0, v).astype(BF16)
        ua = proj_s[rows, UA_OFF:UA_OFF + CONV_WIDTH]
        ug = proj_s[rows, UG_OFF:UG_OFF + CONV_WIDTH]
        z = ua / (1.0 + jnp.exp(-ug))
        for cg in range(CONV_GROUPS):
            z_s[cg, pl.ds(2 * (r0 + ZPAD), BLOCK, stride=2), :] = z[:, cg * LANES:(cg + 1) * LANES]

    qi = lax.broadcasted_iota(jnp.int32, (BLOCK, 2 * BLOCK), 0)
    kj = lax.broadcasted_iota(jnp.int32, (BLOCK, 2 * BLOCK), 1)
    local = ((kj < BLOCK) & (kj > qi)) | ((kj >= BLOCK) & (kj - BLOCK <= qi))
    row_head = (0, 2, 1, 3)

    for j in range(nblk):
        r0 = j * BLOCK
        rows = pl.ds(r0, BLOCK)
        keys = pl.ds(r0, 2 * BLOCK)
        first_key = jnp.where(i * nblk + j == 0, BLOCK, 0)
        mask = local & (kj >= first_key)
        for g in range(N_KV_HEADS):
            c0 = g * 2 * LANES
            qs = jnp.concatenate([qa_s[rows, c0:c0 + LANES], qa_s[rows, c0 + LANES:c0 + 2 * LANES],
                                  qb_s[rows, c0:c0 + LANES], qb_s[rows, c0 + LANES:c0 + 2 * LANES]], axis=0)
            s = lax.dot_general(qs, k2_s[g, keys, :], (((1,), (1,)), ((), ())),
                                preferred_element_type=F32)
            ps, invs = [], []
            for hb in range(GROUP):
                sink = sinks_ref[g * GROUP + row_head[hb]]
                sb = jnp.where(mask, s[hb * BLOCK:(hb + 1) * BLOCK], -jnp.inf)
                m = jnp.maximum(jnp.max(sb, axis=-1, keepdims=True), sink)
                p = jnp.exp(sb - m)
                invs.append(1.0 / (jnp.sum(p, axis=-1, keepdims=True) + jnp.exp(sink - m)))
                ps.append(p.astype(BF16))
            o = (jnp.dot(jnp.concatenate(ps[0:2], axis=0), va_s[g, keys, :], preferred_element_type=F32)
                 + jnp.dot(jnp.concatenate(ps[2:4], axis=0), vb_s[g, keys, :], preferred_element_type=F32))
            for pr in range(2):
                cols = slice(c0 + pr * LANES, c0 + (pr + 1) * LANES)
                inv = jnp.where(head_a, invs[pr], invs[2 + pr])
                ga = proj_s[rows, GA_OFF + c0 + pr * LANES:GA_OFF + c0 + (pr + 1) * LANES]
                y_s[rows, cols] = (o[pr * BLOCK:(pr + 1) * BLOCK] * inv * _silu(ga)).astype(BF16)

    cbias = jnp.broadcast_to(cb_ref[...], (SUBLANES, CONV_WIDTH))
    pieces = CONV_ROWS // SUBLANES

    for c in range(TS // CONV_ROWS):
        r0 = c * CONV_ROWS
        groups = []
        for cg in range(CONV_GROUPS):
            cols = slice(cg * LANES, (cg + 1) * LANES)
            acc = [cbias[:, cols]] * pieces
            for k in range(CONV_KERNEL):
                wk = wb_s[k, :, cols]
                for r in range(pieces):
                    t0 = r0 + r * SUBLANES + ZSHIFT + k
                    acc[r] = acc[r] + z_s[cg, pl.ds(2 * t0, SUBLANES, stride=2), :] * wk
            groups.append(jnp.concatenate(acc, axis=0))
        zc = jnp.concatenate(groups, axis=1)
        mu = jnp.mean(zc, axis=-1, keepdims=True)
        d = zc - mu
        var = jnp.mean(d * d, axis=-1, keepdims=True)
        ln = d * lax.rsqrt(var + EPS) * lnw_ref[...] + lnb_ref[...]
        rows = pl.ds(r0, CONV_ROWS)
        gb = proj_s[rows, GB_OFF:GB_OFF + CONV_WIDTH]
        y_s[rows, ATTN_WIDTH:ATTN_WIDTH + CONV_WIDTH] = (_silu(ln) * _silu(gb)).astype(BF16)

    for n in range(D_MODEL // MXU_N):
        cols = slice(n * MXU_N, (n + 1) * MXU_N)
        out = jnp.dot(y_s[...], wout_ref[:, cols], preferred_element_type=F32)
        o_ref[:, cols] = x_ref[:, cols] + gate[:, cols] * out

    k2_s[:, 0:BLOCK, :] = k2_s[:, TS:TS + BLOCK, :]
    va_s[:, 0:BLOCK, :] = va_s[:, TS:TS + BLOCK, :]
    vb_s[:, 0:BLOCK, :] = vb_s[:, TS:TS + BLOCK, :]
    for cg in range(CONV_GROUPS):
        z_s[cg, pl.ds(0, ZPAD, stride=2), :] = z_s[cg, pl.ds(2 * TS, ZPAD, stride=2), :]


def _rope_tables(seq_len):
    inv = ROPE_THETA ** (-jnp.arange(0, HEAD_DIM, 2, dtype=F32) / HEAD_DIM)
    ang = jnp.arange(seq_len, dtype=F32)[:, None] * inv[None, :]
    cos, sin = jnp.cos(ang), jnp.sin(ang)
    cos_l = jnp.tile(cos, (1, 2 * LANES // HEAD_DIM))
    sin_l = jnp.tile(jnp.concatenate([-sin, sin], axis=-1), (1, LANES // HEAD_DIM))
    return cos_l, sin_l


def kernel(x, c, w_ada, b_ada, norm_w, w_in, q_norm_w, k_norm_w, sinks, conv_w, conv_b, ln_w, ln_b, w_out):
    bsz, seq, d = x.shape
    assert d == D_MODEL and seq % TS == 0 and w_ada.shape[0] == 1
    cos_l, sin_l = _rope_tables(seq)
    mod = _adaln(c, w_ada[0], b_ada[0]).reshape(bsz, 3, d)
    win = w_in[0].astype(BF16)
    wout = w_out[0].astype(BF16)
    qw = jnp.tile(q_norm_w[0], N_Q_HEADS).reshape(1, ATTN_WIDTH)
    kw = jnp.tile(k_norm_w[0], N_KV_HEADS).reshape(1, KV_WIDTH)

    const = lambda shape: pl.BlockSpec(shape, lambda b, i: (0,) * len(shape))
    return pl.pallas_call(
        _layer_kernel,
        grid=(bsz, seq // TS),
        in_specs=[
            pl.BlockSpec(memory_space=pltpu.SMEM),
            pl.BlockSpec((None, TS, d), lambda b, i: (b, i, 0)),
            pl.BlockSpec((None, 3, d), lambda b, i: (b, 0, 0)),
            const((1, d)),
            const((d, IN_WIDTH)),
            const((1, ATTN_WIDTH)),
            const((1, KV_WIDTH)),
            pl.BlockSpec((TS, LANES), lambda b, i: (i, 0)),
            pl.BlockSpec((TS, LANES), lambda b, i: (i, 0)),
            const((CONV_KERNEL, CONV_WIDTH)),
            const((1, CONV_WIDTH)),
            const((1, CONV_WIDTH)),
            const((1, CONV_WIDTH)),
            const((d, d)),
        ],
        out_specs=pl.BlockSpec((None, TS, d), lambda b, i: (b, i, 0)),
        out_shape=jax.ShapeDtypeStruct((bsz, seq, d), F32),
        scratch_shapes=[
            pltpu.VMEM((TS, d), BF16),
            pltpu.VMEM((TS, IN_WIDTH), F32),
            pltpu.VMEM((TS, ATTN_WIDTH), BF16),
            pltpu.VMEM((TS, ATTN_WIDTH), BF16),
            pltpu.VMEM((N_KV_HEADS, TS + BLOCK, LANES), BF16),
            pltpu.VMEM((N_KV_HEADS, TS + BLOCK, LANES), BF16),
            pltpu.VMEM((N_KV_HEADS, TS + BLOCK, LANES), BF16),
            pltpu.VMEM((TS, d), BF16),
            pltpu.VMEM((CONV_GROUPS, 2 * (TS + ZPAD), LANES), F32),
            pltpu.VMEM((CONV_KERNEL, SUBLANES, CONV_WIDTH), F32),
            pltpu.VMEM((ATTN_WIDTH, ATTN_WIDTH), BF16),
        ],
        compiler_params=pltpu.CompilerParams(
            dimension_semantics=("arbitrary", "arbitrary"),
            vmem_limit_bytes=VMEM_LIMIT_BYTES),
        name="hybrid_layer",
    )(sinks[0], x, mod, norm_w, win, qw, kw, cos_l, sin_l, conv_w[0], conv_b, ln_w, ln_b, wout)
```
